```python
import math
import jax, jax.numpy as jnp
from jax import lax
import numpy as np

D_MODEL = 2048
BATCH = 4
SEQ = 4096
DEPTH = 2

MIX_WIDTH = D_MODEL
RET_HEADS = 4
RET_WIDTH = MIX_WIDTH // 4
RET_DIM = RET_WIDTH // RET_HEADS
RET_CHUNK = 128
RET_ROPE_THETA = 10000.0
DIFF_HEADS = 4
DIFF_WIDTH = MIX_WIDTH // 2
DIFF_V_DIM = DIFF_WIDTH // DIFF_HEADS
DIFF_QK_DIM = DIFF_V_DIM // 2
DIFF_ROT_DIM = DIFF_QK_DIM // 4
ROPE_THETA = 500000.0
Q_BLOCK = 128
MLSTM_HEADS = 4
MLSTM_WIDTH = MIX_WIDTH // 4
MLSTM_DIM = MLSTM_WIDTH // MLSTM_HEADS
MLSTM_CHUNK = 64
CONV_K = 4
D_FF = ((8 * D_MODEL // 3 + 127) // 128) * 128
ALPHA = (2 * DEPTH) ** 0.25
BETA = (8 * DEPTH) ** -0.25
PROJ_SPLITS = (
    RET_WIDTH, RET_WIDTH, RET_WIDTH, RET_WIDTH,
    2 * DIFF_HEADS * DIFF_QK_DIM, 2 * DIFF_HEADS * DIFF_QK_DIM,
    DIFF_WIDTH,
    MLSTM_WIDTH, MLSTM_WIDTH, MLSTM_WIDTH, MLSTM_WIDTH,
    MLSTM_HEADS, MLSTM_HEADS,
)
PROJ_WIDTH = sum(PROJ_SPLITS)
PROJ_OFFSETS = tuple(int(o) for o in np.cumsum(PROJ_SPLITS)[:-1])

kernel_name = "hymba_style_retention_diffattn_mlstm_macaron_deepnorm"

F32 = jnp.float32


def layer_norm(x, g, b, eps=1e-5):
    xf = x.astype(F32)
    mu = jnp.mean(xf, axis=-1, keepdims=True)
    xc = xf - mu
    var = jnp.mean(xc * xc, axis=-1, keepdims=True)
    return (xc * lax.rsqrt(var + eps) * g + b).astype(x.dtype)


def head_norm(t, g, rms, eps=1e-6):
    if rms:
        y = t * lax.rsqrt(jnp.mean(t * t, axis=-1, keepdims=True) + eps)
    else:
        tc = t - jnp.mean(t, axis=-1, keepdims=True)
        y = tc * lax.rsqrt(jnp.mean(tc * tc, axis=-1, keepdims=True) + eps)
    return y * g


def swiglu(x, w_gu, w_down):
    a, u = jnp.split(x @ w_gu, 2, axis=-1)
    return (jax.nn.silu(a) * u) @ w_down


def rope_tables(seq, rot_dim, theta):
    pos = jnp.arange(seq, dtype=F32)
    inv = theta ** (-jnp.arange(0, rot_dim, 2, dtype=F32) / rot_dim)
    ang = pos[:, None] * inv[None, :]
    return jnp.cos(ang), jnp.sin(ang)


def apply_rotary(t, cos, sin, rot_dim):
    half = rot_dim // 2
    t1, t2, tp = t[..., :half], t[..., half:rot_dim], t[..., rot_dim:]
    return jnp.concatenate([t1 * cos - t2 * sin, t1 * sin + t2 * cos, tp], axis=-1).astype(t.dtype)


def split_heads(t, n_heads):
    B, S, W = t.shape
    return t.reshape(B, S, n_heads, W // n_heads).transpose(0, 2, 1, 3)


def merge_heads(t):
    B, H, S, d = t.shape
    return t.transpose(0, 2, 1, 3).reshape(B, S, H * d)


def to_chunks(t, c):
    B, H, S = t.shape[:3]
    t = t.reshape((B, H, S // c, c) + t.shape[3:])
    return jnp.moveaxis(t, 2, 0)


def from_chunks(t):
    n, B, H, c, d = t.shape
    return jnp.moveaxis(t, 0, 2).reshape(B, H, n * c, d)


def retention_chunkwise(q, k, v):
    B, H, S, d = q.shape
    C = RET_CHUNK
    k = k * (d ** -0.5)
    log_g = jnp.log(1.0 - jnp.exp2(-5.0 - jnp.arange(H, dtype=F32)))
    idx = jnp.arange(C, dtype=F32)
    rel = idx[:, None] - idx[None, :]
    decay_in = jnp.where(rel >= 0, jnp.exp(jnp.maximum(rel, 0.0) * log_g[:, None, None]), 0.0)
    xi = jnp.exp((idx + 1.0)[None, :] * log_g[:, None])[None, :, :, None]
    zeta = jnp.exp((C - 1.0 - idx)[None, :] * log_g[:, None])[None, :, :, None]
    g_chunk = jnp.exp(C * log_g)[None, :, None, None]

    def step(R, inp):
        qc, kc, vc = inp
        inner = jnp.einsum('bhid,bhjd->bhij', qc, kc) * decay_in
        y = (jnp.einsum('bhij,bhjv->bhiv', inner, vc)
             + jnp.einsum('bhid,bhdv->bhiv', qc, R) * xi)
        R = g_chunk * R + jnp.einsum('bhjd,bhjv->bhdv', kc * zeta, vc)
        return R, y

    R0 = jnp.zeros((B, H, d, v.shape[-1]), F32)
    _, ys = lax.scan(step, R0, (to_chunks(q, C), to_chunks(k, C), to_chunks(v, C)))
    return from_chunks(ys)


def diff_attention(q, k, v, lam):
    B, H, _, S, dq = q.shape
    nb = S // Q_BLOCK
    scale = dq ** -0.5
    qb = jnp.moveaxis(q.reshape(B, H, 2, nb, Q_BLOCK, dq), 3, 0)
    kpos = jnp.arange(S)
    vf = v.astype(F32)

    def block(args):
        qblk, bi = args
        s = jnp.einsum('bhmqd,bhmkd->bhmqk', qblk, k).astype(F32) * scale
        qpos = bi * Q_BLOCK + jnp.arange(Q_BLOCK)
        s = jnp.where(kpos[None, :] <= qpos[:, None], s, -jnp.inf)
        p = jax.nn.softmax(s, axis=-1)
        a = p[:, :, 0] - lam * p[:, :, 1]
        return jnp.einsum('bhqk,bhkv->bhqv', a, vf)

    o = lax.map(block, (qb, jnp.arange(nb)))
    return from_chunks(o)


def mlstm_chunkwise(q, k, v, i_pre, f_pre):
    B, H, S, d = q.shape
    L = MLSTM_CHUNK
    k = k * (d ** -0.5)
    log_f = jax.nn.log_sigmoid(f_pre)
    tril = jnp.tril(jnp.ones((L, L), dtype=bool))

    def step(carry, inp):
        Cs, ns, m = carry
        qc, kc, vc, ic, lfc = inp
        b = jnp.cumsum(lfc, axis=-1)
        a = b + m[..., None]
        Dm = jnp.where(tril, b[..., :, None] - b[..., None, :] + ic[..., None, :], -jnp.inf)
        mt = jnp.maximum(a, jnp.max(Dm, axis=-1))
        sc = jnp.einsum('bhid,bhjd->bhij', qc, kc) * jnp.exp(Dm - mt[..., None])
        inter = jnp.exp(a - mt)
        num = (jnp.einsum('bhij,bhjv->bhiv', sc, vc)
               + inter[..., None] * jnp.einsum('bhid,bhdv->bhiv', qc, Cs))
        den = jnp.sum(sc, axis=-1) + inter * jnp.einsum('bhid,bhd->bhi', qc, ns)
        h = num / jnp.maximum(jnp.abs(den), jnp.exp(-mt))[..., None]
        bL = b[..., -1]
        w_log = bL[..., None] - b + ic
        m_new = jnp.maximum(bL + m, jnp.max(w_log, axis=-1))
        dec = jnp.exp(bL + m - m_new)
        kw = kc * jnp.exp(w_log - m_new[..., None])[..., None]
        Cs = dec[..., None, None] * Cs + jnp.einsum('bhjd,bhjv->bhdv', kw, vc)
        ns = dec[..., None] * ns + jnp.sum(kw, axis=2)
        return (Cs, ns, m_new), h

    init = (jnp.zeros((B, H, d, d), F32), jnp.zeros((B, H, d), F32), jnp.zeros((B, H), F32))
    _, hs = lax.scan(step, init, (to_chunks(q, L), to_chunks(k, L), to_chunks(v, L),
                                  to_chunks(i_pre, L), to_chunks(log_f, L)))
    return from_chunks(hs)


def causal_dwconv_silu(u, w, b):
    S = u.shape[1]
    up = jnp.pad(u, ((0, 0), (CONV_K - 1, 0), (0, 0)))
    y = b + sum(up[:, j:j + S, :] * w[j] for j in range(CONV_K))
    return jax.nn.silu(y)


def token_mixer(h, w_in, ret_norm_g, diff_lambda, diff_norm_g, conv_w, conv_b, gate_b,
                mlstm_norm_g, w_out, lam_init, ret_cs, diff_cs):
    B, S, _ = h.shape
    proj = h @ w_in
    (rq, rk, rv, rg, dq, dk, dv, mq, mk, mv, mo, mi, mf) = jnp.split(proj, PROJ_OFFSETS, axis=-1)

    rcos, rsin = ret_cs
    rq = apply_rotary(split_heads(rq, RET_HEADS).astype(F32), rcos, rsin, RET_DIM)
    rk = apply_rotary(split_heads(rk, RET_HEADS).astype(F32), rcos, rsin, RET_DIM)
    ry = retention_chunkwise(rq, rk, split_heads(rv, RET_HEADS).astype(F32))
    ry = head_norm(ry, ret_norm_g.reshape(RET_HEADS, 1, RET_DIM), rms=False)
    ret_out = jax.nn.silu(rg.astype(F32)) * merge_heads(ry)

    dcos, dsin = diff_cs
    def qk_heads(t):
        t = t.reshape(B, S, DIFF_HEADS, 2, DIFF_QK_DIM).transpose(0, 2, 3, 1, 4)
        return apply_rotary(t.astype(F32), dcos, dsin, DIFF_ROT_DIM)
    dl = diff_lambda.astype(F32)
    lam = (jnp.exp(jnp.sum(dl[0] * dl[1])) - jnp.exp(jnp.sum(dl[2] * dl[3])) + lam_init)
    dy = diff_attention(qk_heads(dq), qk_heads(dk), split_heads(dv, DIFF_HEADS), lam)
    dy = head_norm(dy, diff_norm_g, rms=True) * (1.0 - lam_init)
    diff_out = merge_heads(dy)

    mqk = causal_dwconv_silu(jnp.concatenate([mq, mk], axis=-1), conv_w, conv_b)
    mq_c, mk_c = jnp.split(mqk.astype(F32), 2, axis=-1)
    i_pre = (mi.astype(F32) + gate_b[0]).transpose(0, 2, 1)
    f_pre = (mf.astype(F32) + gate_b[1]).transpose(0, 2, 1)
    mh = mlstm_chunkwise(split_heads(mq_c, MLSTM_HEADS), split_heads(mk_c, MLSTM_HEADS),
                         split_heads(mv, MLSTM_HEADS).astype(F32), i_pre, f_pre)
    mh = jax.nn.sigmoid(split_heads(mo, MLSTM_HEADS).astype(F32)) * mh
    mh = head_norm(mh, mlstm_norm_g.reshape(MLSTM_HEADS, 1, MLSTM_DIM), rms=False)
    mlstm_out = merge_heads(mh)

    mixed = jnp.concatenate([ret_out, diff_out, mlstm_out], axis=-1).astype(h.dtype)
    return mixed @ w_out


def setup_inputs(seed: int = 0) -> dict:
    key = jax.random.key(seed)
    ks = jax.random.split(key, 20)
    nrm = lambda k, shape, s: jax.random.normal(k, shape, F32) * s
    gate_b = jnp.stack([
        nrm(ks[12], (DEPTH, MLSTM_HEADS), 0.1),
        jnp.linspace(3.0, 6.0, MLSTM_HEADS, dtype=F32)[None, :] + nrm(ks[13], (DEPTH, MLSTM_HEADS), 0.1),
    ], axis=1)
    return {
        "x": nrm(ks[0], (BATCH, SEQ, D_MODEL), 1.0),
        "ln_g": 1.0 + nrm(ks[1], (DEPTH, 3, D_MODEL), 0.02),
        "ln_b": nrm(ks[2], (DEPTH, 3, D_MODEL), 0.02),
        "ffn1_w_gu": nrm(ks[3], (DEPTH, D_MODEL, 2 * D_FF), D_MODEL ** -0.5),
        "ffn1_w_down": nrm(ks[4], (DEPTH, D_FF, D_MODEL), BETA * D_FF ** -0.5),
        "w_in": nrm(ks[5], (DEPTH, D_MODEL, PROJ_WIDTH), D_MODEL ** -0.5),
        "ret_norm_g": 1.0 + nrm(ks[6], (DEPTH, RET_WIDTH), 0.02),
        "diff_lambda": nrm(ks[7], (DEPTH, 4, DIFF_QK_DIM), 0.1),
        "diff_norm_g": 1.0 + nrm(ks[8], (DEPTH, DIFF_V_DIM), 0.02),
        "mlstm_conv_w": nrm(ks[9], (DEPTH, CONV_K, 2 * MLSTM_WIDTH), CONV_K ** -0.5),
        "mlstm_conv_b": nrm(ks[10], (DEPTH, 2 * MLSTM_WIDTH), 0.02),
        "mlstm_gate_b": gate_b,
        "mlstm_norm_g": 1.0 + nrm(ks[11], (DEPTH, MLSTM_WIDTH), 0.02),
        "w_out": nrm(ks[14], (DEPTH, MIX_WIDTH, D_MODEL), BETA * MIX_WIDTH ** -0.5),
        "ffn2_w_gu": nrm(ks[15], (DEPTH, D_MODEL, 2 * D_FF), D_MODEL ** -0.5),
        "ffn2_w_down": nrm(ks[16], (DEPTH, D_FF, D_MODEL), BETA * D_FF ** -0.5),
    }


def reference(x, ln_g, ln_b, ffn1_w_gu, ffn1_w_down, w_in, ret_norm_g, diff_lambda, diff_norm_g,
              mlstm_conv_w, mlstm_conv_b, mlstm_gate_b, mlstm_norm_g, w_out, ffn2_w_gu, ffn2_w_down):
    S = x.shape[1]
    ret_cs = rope_tables(S, RET_DIM, RET_ROPE_THETA)
    diff_cs = rope_tables(S, DIFF_ROT_DIM, ROPE_THETA)
    for l in range(DEPTH):
        lam_init = 0.8 - 0.6 * math.exp(-0.3 * l)
        x = layer_norm(ALPHA * x + 0.5 * swiglu(x, ffn1_w_gu[l], ffn1_w_down[l]), ln_g[l, 0], ln_b[l, 0])
        x = layer_norm(ALPHA * x + token_mixer(x, w_in[l], ret_norm_g[l], diff_lambda[l], diff_norm_g[l],
                                               mlstm_conv_w[l], mlstm_conv_b[l], mlstm_gate_b[l],
                                               mlstm_norm_g[l], w_out[l], lam_init, ret_cs, diff_cs),
                       ln_g[l, 1], ln_b[l, 1])
        x = layer_norm(ALPHA * x + 0.5 * swiglu(x, ffn2_w_gu[l], ffn2_w_down[l]), ln_g[l, 2], ln_b[l, 2])
    return x
```

```python
import functools
import math

import jax
import jax.numpy as jnp
from jax import lax
from jax.experimental import pallas as pl
from jax.experimental.pallas import tpu as pltpu

F32 = jnp.float32
BF16 = jnp.bfloat16

D_MODEL = 2048
DEPTH = 2
N_HEADS = 4
HEAD_DIM = 128
RET_WIDTH = N_HEADS * HEAD_DIM
DIFF_V_DIM = 2 * HEAD_DIM
DIFF_WIDTH = N_HEADS * DIFF_V_DIM
DIFF_ROT_DIM = HEAD_DIM // 4
MLSTM_WIDTH = N_HEADS * HEAD_DIM
RET_ROPE_THETA = 10000.0
ROPE_THETA = 500000.0
CONV_K = 4
D_FF = ((8 * D_MODEL // 3 + 127) // 128) * 128
ALPHA = (2 * DEPTH) ** 0.25

OFF_RET = 0
OFF_DQ = 4 * RET_WIDTH
OFF_DK = OFF_DQ + DIFF_WIDTH
OFF_DV = OFF_DK + DIFF_WIDTH
OFF_MQK = OFF_DV + DIFF_WIDTH
OFF_MV = OFF_MQK + 2 * MLSTM_WIDTH
OFF_MO = OFF_MV + MLSTM_WIDTH
OFF_GATES = OFF_MO + MLSTM_WIDTH
PROJ_MAIN = OFF_GATES

LANES = 128
VMEM_LIMIT = 56 * 1024 * 1024

FFN_TM = 512
FFN_TF = 512
D_FF_PAD = ((D_FF + FFN_TF - 1) // FFN_TF) * FFN_TF
PROJ_TM = 1024
PROJ_TN = 1024
OUT_TM = 512
SEQ_TILE = 512
RET_CHUNK = 128
MLSTM_CHUNK = 128
ATT_TQ = 512
ATT_TK = 512

_NT = (((1,), (1,)), ((), ()))
_TN = (((0,), (0,)), ((), ()))


def _dot(a, b):
    return jnp.dot(a, b, preferred_element_type=F32)


def _dot_nt(a, b):
    return lax.dot_general(a, b, _NT, preferred_element_type=F32)


def _dot_tn(a, b):
    return lax.dot_general(a, b, _TN, preferred_element_type=F32)


def _sigmoid(x):
    return 1.0 / (1.0 + jnp.exp(-x))


def _log_sigmoid(x):
    return jnp.minimum(x, 0.0) - jnp.log(1.0 + jnp.exp(-jnp.abs(x)))


def _layer_norm(y, g, b, eps):
    mu = jnp.mean(y, axis=-1, keepdims=True)
    yc = y - mu
    var = jnp.mean(yc * yc, axis=-1, keepdims=True)
    return yc * lax.rsqrt(var + eps) * g + b


def _params(*sem):
    return pltpu.CompilerParams(dimension_semantics=sem, vmem_limit_bytes=VMEM_LIMIT)


def _ffn_ln_kernel(x_ref, wg_ref, wu_ref, wd_ref, g_ref, b_ref, o_ref, xb_ref):
    j = pl.program_id(1)

    @pl.when(j == 0)
    def _():
        xb_ref[...] = x_ref[...].astype(BF16)
        o_ref[...] = jnp.zeros_like(o_ref)

    xb = xb_ref[...]
    a = _dot(xb, wg_ref[...])
    u = _dot(xb, wu_ref[...])
    h = (a * _sigmoid(a) * u).astype(BF16)
    o_ref[...] += _dot(h, wd_ref[...])

    @pl.when(j == pl.num_programs(1) - 1)
    def _():
        y = ALPHA * x_ref[...] + 0.5 * o_ref[...]
        o_ref[...] = _layer_norm(y, g_ref[...], b_ref[...], 1e-5)


def _ffn_ln(x, wg, wu, wd, g, b):
    t, d = x.shape
    nf = wg.shape[1] // FFN_TF
    return pl.pallas_call(
        _ffn_ln_kernel,
        grid=(t // FFN_TM, nf),
        in_specs=[
            pl.BlockSpec((FFN_TM, d), lambda i, j: (i, 0)),
            pl.BlockSpec((d, FFN_TF), lambda i, j: (0, j)),
            pl.BlockSpec((d, FFN_TF), lambda i, j: (0, j)),
            pl.BlockSpec((FFN_TF, d), lambda i, j: (j, 0)),
            pl.BlockSpec((1, d), lambda i, j: (0, 0)),
            pl.BlockSpec((1, d), lambda i, j: (0, 0)),
        ],
        out_specs=pl.BlockSpec((FFN_TM, d), lambda i, j: (i, 0)),
        out_shape=jax.ShapeDtypeStruct((t, d), F32),
        scratch_shapes=[pltpu.VMEM((FFN_TM, d), BF16)],
        compiler_params=_params("parallel", "arbitrary"),
        name="ffn_ln",
    )(x, wg, wu, wd, g, b)


def _in_proj_kernel(x_ref, w_ref, wgc_ref, wgt_ref, bc_ref, bt_ref, p_ref, gc_ref, gt_ref, xb_ref):
    j = pl.program_id(1)

    @pl.when(j == 0)
    def _():
        xb = x_ref[...].astype(BF16)
        xb_ref[...] = xb
        gc_ref[...] = _dot(xb, wgc_ref[...]) + bc_ref[...]
        gt_ref[...] = _dot_nt(wgt_ref[...], xb) + bt_ref[...]

    p_ref[...] = _dot(xb_ref[...], w_ref[...]).astype(BF16)


def _in_proj(x, w, wgc, wgt, bc, bt):
    t, d = x.shape
    n = w.shape[1]
    return pl.pallas_call(
        _in_proj_kernel,
        grid=(t // PROJ_TM, n // PROJ_TN),
        in_specs=[
            pl.BlockSpec((PROJ_TM, d), lambda i, j: (i, 0)),
            pl.BlockSpec((d, PROJ_TN), lambda i, j: (0, j)),
            pl.BlockSpec((d, LANES), lambda i, j: (0, 0)),
            pl.BlockSpec((8, d), lambda i, j: (0, 0)),
            pl.BlockSpec((1, LANES), lambda i, j: (0, 0)),
            pl.BlockSpec((8, 1), lambda i, j: (0, 0)),
        ],
        out_specs=[
            pl.BlockSpec((PROJ_TM, PROJ_TN), lambda i, j: (i, j)),
            pl.BlockSpec((PROJ_TM, LANES), lambda i, j: (i, 0)),
            pl.BlockSpec((8, PROJ_TM), lambda i, j: (0, i)),
        ],
        out_shape=[
            jax.ShapeDtypeStruct((t, n), BF16),
            jax.ShapeDtypeStruct((t, LANES), F32),
            jax.ShapeDtypeStruct((8, t), F32),
        ],
        scratch_shapes=[pltpu.VMEM((PROJ_TM, d), BF16)],
        compiler_params=_params("parallel", "arbitrary"),
        name="in_proj",
    )(x, w, wgc, wgt, bc, bt)


def _retention_kernel(p_ref, cos_ref, sin_ref, g_ref, o_ref, r_ref):
    s = pl.program_id(1)
    c_len = RET_CHUNK

    @pl.when(s == 0)
    def _():
        r_ref[...] = jnp.zeros_like(r_ref)

    ii = lax.broadcasted_iota(jnp.int32, (c_len, c_len), 0)
    jj = lax.broadcasted_iota(jnp.int32, (c_len, c_len), 1)
    rel = (ii - jj).astype(F32)
    idx = lax.broadcasted_iota(jnp.int32, (c_len, 1), 0).astype(F32)
    for h in range(N_HEADS):
        log_g = math.log(1.0 - 2.0 ** (-5.0 - h))
        decay = jnp.where(rel >= 0, jnp.exp(jnp.maximum(rel, 0.0) * log_g), 0.0)
        xi = jnp.exp((idx + 1.0) * log_g)
        zeta = jnp.exp((c_len - 1.0 - idx) * log_g)
        g_chunk = math.exp(c_len * log_g)
        cols = slice(h * HEAD_DIM, (h + 1) * HEAD_DIM)
        g_h = g_ref[:, cols]
        for c in range(SEQ_TILE // c_len):
            rows = slice(c * c_len, (c + 1) * c_len)
            cosf = cos_ref[rows, :]
            sins = sin_ref[rows, :]
            q = p_ref[rows, h * HEAD_DIM:(h + 1) * HEAD_DIM].astype(F32)
            k = p_ref[rows, RET_WIDTH + h * HEAD_DIM:RET_WIDTH + (h + 1) * HEAD_DIM].astype(F32)
            v = p_ref[rows, 2 * RET_WIDTH + h * HEAD_DIM:2 * RET_WIDTH + (h + 1) * HEAD_DIM]
            gate = p_ref[rows, 3 * RET_WIDTH + h * HEAD_DIM:3 * RET_WIDTH + (h + 1) * HEAD_DIM].astype(F32)
            qr = q * cosf + pltpu.roll(q, HEAD_DIM // 2, 1) * sins
            kr = (k * cosf + pltpu.roll(k, HEAD_DIM // 2, 1) * sins) * (HEAD_DIM ** -0.5)
            qb = qr.astype(BF16)
            inner = _dot_nt(qb, kr.astype(BF16)) * decay
            r_old = r_ref[h]
            y = _dot(inner.astype(BF16), v) + _dot(qb, r_old.astype(BF16)) * xi
            r_ref[h] = g_chunk * r_old + _dot_tn((kr * zeta).astype(BF16), v)
            yc = y - jnp.mean(y, axis=-1, keepdims=True)
            yn = yc * lax.rsqrt(jnp.mean(yc * yc, axis=-1, keepdims=True) + 1e-6) * g_h
            o_ref[rows, cols] = (gate * _sigmoid(gate) * yn).astype(BF16)


def _retention(proj3, rcos, rsin, norm_g):
    b, s, _ = proj3.shape
    return pl.pallas_call(
        _retention_kernel,
        grid=(b, s // SEQ_TILE),
        in_specs=[
            pl.BlockSpec((None, SEQ_TILE, 4 * RET_WIDTH), lambda i, j: (i, j, OFF_RET // (4 * RET_WIDTH))),
            pl.BlockSpec((SEQ_TILE, HEAD_DIM), lambda i, j: (j, 0)),
            pl.BlockSpec((SEQ_TILE, HEAD_DIM), lambda i, j: (j, 0)),
            pl.BlockSpec((1, RET_WIDTH), lambda i, j: (0, 0)),
        ],
        out_specs=pl.BlockSpec((None, SEQ_TILE, RET_WIDTH), lambda i, j: (i, j, 0)),
        out_shape=jax.ShapeDtypeStruct((b, s, RET_WIDTH), BF16),
        scratch_shapes=[pltpu.VMEM((N_HEADS, HEAD_DIM, HEAD_DIM), F32)],
        compiler_params=_params("parallel", "arbitrary"),
        name="retention",
    )(proj3, rcos, rsin, norm_g)


def _rotate_partial(t, c, s1, s2):
    half = DIFF_ROT_DIM // 2
    return t * c + pltpu.roll(t, half, 1) * s1 + pltpu.roll(t, HEAD_DIM - half, 1) * s2


def _diff_attn_kernel(lam_init, q_ref, k_ref, v_ref, qc_ref, qs1_ref, qs2_ref, kc_ref, ks1_ref, ks2_ref,
                      lam_ref, g_ref, o_ref, krot_ref, acc_ref, m_ref, l_ref):
    qi = pl.program_id(2)
    seq = k_ref.shape[0]

    @pl.when(qi == 0)
    def _():
        for r in range(seq // ATT_TK):
            rows = slice(r * ATT_TK, (r + 1) * ATT_TK)
            for mp in range(2):
                cols = slice(mp * HEAD_DIM, (mp + 1) * HEAD_DIM)
                t = k_ref[rows, cols].astype(F32)
                krot_ref[rows, cols] = _rotate_partial(
                    t, kc_ref[rows, :], ks1_ref[rows, :], ks2_ref[rows, :]).astype(BF16)

    scale = HEAD_DIM ** -0.5
    qb = []
    for mp in range(2):
        t = q_ref[:, mp * HEAD_DIM:(mp + 1) * HEAD_DIM].astype(F32)
        qb.append((_rotate_partial(t, qc_ref[...], qs1_ref[...], qs2_ref[...]) * scale).astype(BF16))

    m_ref[...] = jnp.full_like(m_ref, -jnp.inf)
    l_ref[...] = jnp.zeros_like(l_ref)
    acc_ref[...] = jnp.zeros_like(acc_ref)

    def step(off, masked):
        kblk = krot_ref[pl.ds(off, ATT_TK), :]
        vblk = v_ref[pl.ds(off, ATT_TK), :]
        for mp in range(2):
            sc = _dot_nt(qb[mp], kblk[:, mp * HEAD_DIM:(mp + 1) * HEAD_DIM])
            if masked:
                row = lax.broadcasted_iota(jnp.int32, sc.shape, 0)
                col = lax.broadcasted_iota(jnp.int32, sc.shape, 1)
                sc = jnp.where(col <= row, sc, -jnp.inf)
            m_old = m_ref[mp]
            m_new = jnp.maximum(m_old, jnp.max(sc, axis=-1, keepdims=True))
            p = jnp.exp(sc - m_new)
            alpha = jnp.exp(m_old - m_new)
            l_ref[mp] = alpha * l_ref[mp] + jnp.sum(p, axis=-1, keepdims=True)
            acc_ref[mp] = alpha * acc_ref[mp] + _dot(p.astype(BF16), vblk)
            m_ref[mp] = m_new

    def body(kb, carry):
        step(pl.multiple_of(kb * ATT_TK, ATT_TK), False)
        return carry

    lax.fori_loop(0, qi, body, 0)
    step(pl.multiple_of(qi * ATT_TK, ATT_TK), True)

    dl = lam_ref[...]
    lam = (jnp.exp(jnp.sum(dl[0:1] * dl[1:2], axis=-1, keepdims=True))
           - jnp.exp(jnp.sum(dl[2:3] * dl[3:4], axis=-1, keepdims=True)) + lam_init)
    o = acc_ref[0] / l_ref[0] - lam * (acc_ref[1] / l_ref[1])
    y = o * lax.rsqrt(jnp.mean(o * o, axis=-1, keepdims=True) + 1e-6) * g_ref[...]
    o_ref[...] = (y * (1.0 - lam_init)).astype(BF16)


def _diff_attention(proj3, tabs, lam_p, norm_g, lam_init):
    b, s, _ = proj3.shape
    dc, ds1, ds2 = tabs
    nq = s // ATT_TQ
    qtab = pl.BlockSpec((ATT_TQ, HEAD_DIM), lambda i, h, q: (q, 0))
    ktab = pl.BlockSpec((s, HEAD_DIM), lambda i, h, q: (0, 0))
    return pl.pallas_call(
        functools.partial(_diff_attn_kernel, lam_init),
        grid=(b, N_HEADS, nq),
        in_specs=[
            pl.BlockSpec((None, ATT_TQ, DIFF_V_DIM), lambda i, h, q: (i, q, OFF_DQ // DIFF_V_DIM + h)),
            pl.BlockSpec((None, s, DIFF_V_DIM), lambda i, h, q: (i, 0, OFF_DK // DIFF_V_DIM + h)),
            pl.BlockSpec((None, s, DIFF_V_DIM), lambda i, h, q: (i, 0, OFF_DV // DIFF_V_DIM + h)),
            qtab, qtab, qtab, ktab, ktab, ktab,
            pl.BlockSpec((4, HEAD_DIM), lambda i, h, q: (0, 0)),
            pl.BlockSpec((1, DIFF_V_DIM), lambda i, h, q: (0, 0)),
        ],
        out_specs=pl.BlockSpec((None, ATT_TQ, DIFF_V_DIM), lambda i, h, q: (i, q, h)),
        out_shape=jax.ShapeDtypeStruct((b, s, DIFF_WIDTH), BF16),
        scratch_shapes=[
            pltpu.VMEM((s, DIFF_V_DIM), BF16),
            pltpu.VMEM((2, ATT_TQ, DIFF_V_DIM), F32),
            pltpu.VMEM((2, ATT_TQ, 1), F32),
            pltpu.VMEM((2, ATT_TQ, 1), F32),
        ],
        compiler_params=_params("parallel", "parallel", "arbitrary"),
        name="diff_attention",
    )(proj3, proj3, proj3, dc, ds1, ds2, dc, ds1, ds2, lam_p, norm_g)


def _mlstm_kernel(qk_ref, v_ref, og_ref, gc_ref, gt_ref, cw_ref, cb_ref, g_ref, o_ref,
                  ext_ref, cs_ref, ns_ref, m_ref):
    s = pl.program_id(1)
    n = MLSTM_CHUNK
    pad = 8

    @pl.when(s == 0)
    def _():
        ext_ref[0:pad, :] = jnp.zeros((pad, 2 * MLSTM_WIDTH), F32)
        cs_ref[...] = jnp.zeros_like(cs_ref)
        ns_ref[...] = jnp.zeros_like(ns_ref)
        m_ref[...] = jnp.zeros_like(m_ref)

    @pl.when(s > 0)
    def _():
        ext_ref[0:pad, :] = ext_ref[SEQ_TILE:SEQ_TILE + pad, :]

    ext_ref[pad:, :] = qk_ref[...].astype(F32)

    gcol = gc_ref[...]
    grow = gt_ref[...]
    lf_col_all = _log_sigmoid(gcol)
    lf_row_all = _log_sigmoid(grow)

    ii = lax.broadcasted_iota(jnp.int32, (n, n), 0)
    jj = lax.broadcasted_iota(jnp.int32, (n, n), 1)
    lower = ii >= jj
    cw = cw_ref[...]
    for c in range(SEQ_TILE // n):
        rows = slice(c * n, (c + 1) * n)
        y = cb_ref[...]
        for j in range(CONV_K):
            y = y + ext_ref[c * n + pad - (CONV_K - 1) + j:c * n + pad - (CONV_K - 1) + j + n, :] * cw[j:j + 1, :]
        act = y * _sigmoid(y)
        for h in range(N_HEADS):
            cols = slice(h * HEAD_DIM, (h + 1) * HEAD_DIM)
            q = act[:, cols]
            k = act[:, MLSTM_WIDTH + h * HEAD_DIM:MLSTM_WIDTH + (h + 1) * HEAD_DIM] * (HEAD_DIM ** -0.5)
            v = v_ref[rows, cols]
            og = og_ref[rows, cols].astype(F32)
            i_col = gcol[rows, h:h + 1]
            i_row = grow[h:h + 1, rows]
            lf_col = lf_col_all[rows, N_HEADS + h:N_HEADS + h + 1]
            lf_row = lf_row_all[N_HEADS + h:N_HEADS + h + 1, rows]

            b_col = jnp.sum(jnp.where(lower, lf_row, 0.0), axis=1, keepdims=True)
            b_row = jnp.sum(jnp.where(ii <= jj, lf_col, 0.0), axis=0, keepdims=True)
            b_last = jnp.sum(lf_row, axis=1, keepdims=True)
            dm = jnp.where(lower, b_col - b_row + i_row, -jnp.inf)
            m_old = m_ref[h]
            a = b_col + m_old
            mt = jnp.maximum(a, jnp.max(dm, axis=1, keepdims=True))
            qb = q.astype(BF16)
            sc = _dot_nt(qb, k.astype(BF16)) * jnp.exp(dm - mt)
            inter = jnp.exp(a - mt)
            cs_old = cs_ref[h]
            ns_old = ns_ref[h]
            num = _dot(sc.astype(BF16), v) + inter * _dot(qb, cs_old.astype(BF16))
            den = jnp.sum(sc, axis=1, keepdims=True) + inter * jnp.sum(q * ns_old, axis=1, keepdims=True)
            hh = num / jnp.maximum(jnp.abs(den), jnp.exp(-mt))

            w_log = b_last - b_col + i_col
            m_new = jnp.maximum(b_last + m_old, jnp.max(w_log, axis=0, keepdims=True))
            dec = jnp.exp(b_last + m_old - m_new)
            kw = k * jnp.exp(w_log - m_new)
            cs_ref[h] = dec * cs_old + _dot_tn(kw.astype(BF16), v)
            ns_ref[h] = dec * ns_old + jnp.sum(kw, axis=0, keepdims=True)
            m_ref[h] = m_new

            hg = _sigmoid(og) * hh
            hc = hg - jnp.mean(hg, axis=-1, keepdims=True)
            hn = hc * lax.rsqrt(jnp.mean(hc * hc, axis=-1, keepdims=True) + 1e-6) * g_ref[:, cols]
            o_ref[rows, cols] = hn.astype(BF16)


def _mlstm(proj3, gcol3, gt, conv_w, conv_b, norm_g):
    b, s, _ = proj3.shape
    ns = s // SEQ_TILE
    return pl.pallas_call(
        _mlstm_kernel,
        grid=(b, ns),
        in_specs=[
            pl.BlockSpec((None, SEQ_TILE, 2 * MLSTM_WIDTH), lambda i, j: (i, j, OFF_MQK // (2 * MLSTM_WIDTH))),
            pl.BlockSpec((None, SEQ_TILE, MLSTM_WIDTH), lambda i, j: (i, j, OFF_MV // MLSTM_WIDTH)),
            pl.BlockSpec((None, SEQ_TILE, MLSTM_WIDTH), lambda i, j: (i, j, OFF_MO // MLSTM_WIDTH)),
            pl.BlockSpec((None, SEQ_TILE, LANES), lambda i, j: (i, j, 0)),
            pl.BlockSpec((8, SEQ_TILE), lambda i, j: (0, i * ns + j)),
            pl.BlockSpec((CONV_K, 2 * MLSTM_WIDTH), lambda i, j: (0, 0)),
            pl.BlockSpec((1, 2 * MLSTM_WIDTH), lambda i, j: (0, 0)),
            pl.BlockSpec((1, MLSTM_WIDTH), lambda i, j: (0, 0)),
        ],
        out_specs=pl.BlockSpec((None, SEQ_TILE, MLSTM_WIDTH), lambda i, j: (i, j, 0)),
        out_shape=jax.ShapeDtypeStruct((b, s, MLSTM_WIDTH), BF16),
        scratch_shapes=[
            pltpu.VMEM((SEQ_TILE + 8, 2 * MLSTM_WIDTH), F32),
            pltpu.VMEM((N_HEADS, HEAD_DIM, HEAD_DIM), F32),
            pltpu.VMEM((N_HEADS, 1, HEAD_DIM), F32),
            pltpu.VMEM((N_HEADS, 1, 1), F32),
        ],
        compiler_params=_params("parallel", "arbitrary"),
        name="mlstm",
    )(proj3, proj3, proj3, gcol3, gt, conv_w, conv_b, norm_g)


def _outproj_ln_kernel(x_ref, r_ref, d_ref, m_ref, w_ref, g_ref, b_ref, o_ref):
    y = _dot(r_ref[...], w_ref[0:RET_WIDTH, :])
    y = y + _dot(d_ref[...], w_ref[RET_WIDTH:RET_WIDTH + DIFF_WIDTH, :])
    y = y + _dot(m_ref[...], w_ref[RET_WIDTH + DIFF_WIDTH:, :])
    o_ref[...] = _layer_norm(ALPHA * x_ref[...] + y, g_ref[...], b_ref[...], 1e-5)


def _outproj_ln(x, ret, dif, ml, w, g, b):
    t, d = x.shape
    return pl.pallas_call(
        _outproj_ln_kernel,
        grid=(t // OUT_TM,),
        in_specs=[
            pl.BlockSpec((OUT_TM, d), lambda i: (i, 0)),
            pl.BlockSpec((OUT_TM, RET_WIDTH), lambda i: (i, 0)),
            pl.BlockSpec((OUT_TM, DIFF_WIDTH), lambda i: (i, 0)),
            pl.BlockSpec((OUT_TM, MLSTM_WIDTH), lambda i: (i, 0)),
            pl.BlockSpec((d, d), lambda i: (0, 0)),
            pl.BlockSpec((1, d), lambda i: (0, 0)),
            pl.BlockSpec((1, d), lambda i: (0, 0)),
        ],
        out_specs=pl.BlockSpec((OUT_TM, d), lambda i: (i, 0)),
        out_shape=jax.ShapeDtypeStruct((t, d), F32),
        compiler_params=_params("parallel"),
        name="outproj_ln",
    )(x, ret, dif, ml, w, g, b)


def _rope_angles(seq, rot_dim, theta):
    pos = jnp.arange(seq, dtype=F32)
    inv = theta ** (-jnp.arange(0, rot_dim, 2, dtype=F32) / rot_dim)
    ang = pos[:, None] * inv[None, :]
    return jnp.cos(ang), jnp.sin(ang)


def _ffn_weights(w_gu, w_down):
    padc = ((0, 0), (0, D_FF_PAD - D_FF))
    wg = jnp.pad(w_gu[:, :D_FF].astype(BF16), padc)
    wu = jnp.pad(w_gu[:, D_FF:].astype(BF16), padc)
    wd = jnp.pad(w_down.astype(BF16), ((0, D_FF_PAD - D_FF), (0, 0)))
    return wg, wu, wd


def kernel(x, ln_g, ln_b, ffn1_w_gu, ffn1_w_down, w_in, ret_norm_g, diff_lambda, diff_norm_g,
           mlstm_conv_w, mlstm_conv_b, mlstm_gate_b, mlstm_norm_g, w_out, ffn2_w_gu, ffn2_w_down):
    bsz, seq, d = x.shape
    t = bsz * seq

    rc, rs = _rope_angles(seq, HEAD_DIM, RET_ROPE_THETA)
    rcos = jnp.concatenate([rc, rc], axis=-1)
    rsin = jnp.concatenate([-rs, rs], axis=-1)
    dc, ds = _rope_angles(seq, DIFF_ROT_DIM, ROPE_THETA)
    half = DIFF_ROT_DIM // 2
    rest = HEAD_DIM - DIFF_ROT_DIM
    zeros_h = jnp.zeros((seq, half), F32)
    tab_c = jnp.concatenate([dc, dc, jnp.ones((seq, rest), F32)], axis=-1)
    tab_s1 = jnp.concatenate([zeros_h, ds, jnp.zeros((seq, rest), F32)], axis=-1)
    tab_s2 = jnp.concatenate([-ds, zeros_h, jnp.zeros((seq, rest), F32)], axis=-1)

    xt = x.reshape(t, d)
    for l in range(DEPTH):
        lam_init = 0.8 - 0.6 * math.exp(-0.3 * l)
        row = lambda v: v.reshape(1, -1)

        wg, wu, wd = _ffn_weights(ffn1_w_gu[l], ffn1_w_down[l])
        xt = _ffn_ln(xt, wg, wu, wd, row(ln_g[l, 0]), row(ln_b[l, 0]))

        w_main = w_in[l][:, :PROJ_MAIN].astype(BF16)
        w_gates = w_in[l][:, PROJ_MAIN:].astype(BF16)
        wgc = jnp.pad(w_gates, ((0, 0), (0, LANES - 2 * N_HEADS)))
        gate_b = mlstm_gate_b[l].reshape(2 * N_HEADS)
        bc = jnp.pad(gate_b, (0, LANES - 2 * N_HEADS)).reshape(1, LANES)
        proj, gcol, gt = _in_proj(xt, w_main, wgc, w_gates.T, bc, gate_b.reshape(2 * N_HEADS, 1))
        proj3 = proj.reshape(bsz, seq, PROJ_MAIN)

        ret = _retention(proj3, rcos, rsin, row(ret_norm_g[l]))
        dif = _diff_attention(proj3, (tab_c, tab_s1, tab_s2), diff_lambda[l], row(diff_norm_g[l]), lam_init)
        ml = _mlstm(proj3, gcol.reshape(bsz, seq, LANES), gt, mlstm_conv_w[l], row(mlstm_conv_b[l]),
                    row(mlstm_norm_g[l]))

        xt = _outproj_ln(xt, ret.reshape(t, RET_WIDTH), dif.reshape(t, DIFF_WIDTH), ml.reshape(t, MLSTM_WIDTH),
                         w_out[l].astype(BF16), row(ln_g[l, 1]), row(ln_b[l, 1]))

        wg, wu, wd = _ffn_weights(ffn2_w_gu[l], ffn2_w_down[l])
        xt = _ffn_ln(xt, wg, wu, wd, row(ln_g[l, 2]), row(ln_b[l, 2]))
    return xt.reshape(bsz, seq, d)
```
